```python
import math
import jax, jax.numpy as jnp
from jax import lax
import numpy as np

D_MODEL = 2048
BATCH = 4
SEQ = 4096
DEPTH = 2

GRID_W = 64
CTX_LEN = 256
N_MIXERS = 2
SSM_EXPAND = 2
D_INNER = SSM_EXPAND * D_MODEL
SSM_HEAD_DIM = 64
SSM_HEADS = D_INNER // SSM_HEAD_DIM
SSM_GROUPS = 8
SSM_STATE = 128
SSM_CONV = 5
SSM_CHUNK = 128
SSM_BC = SSM_GROUPS * SSM_STATE
SSM_CONV_DIM = D_INNER + 4 * SSM_BC
SSM_IN_DIM = D_INNER + SSM_CONV_DIM + 2 * SSM_HEADS
DT_MIN = 1e-3
DT_MAX = 0.1
ATTN_HEADS = 8
ATTN_HEAD_DIM = D_MODEL // ATTN_HEADS // 2
ATTN_Q_BLOCK = 128
ROPE_THETA = 10000.0
MLP_HIDDEN = 4 * D_MODEL
DEEPNORM_ALPHA = (2 * DEPTH) ** 0.25
DEEPNORM_BETA = (8 * DEPTH) ** -0.25
N_SSM_LAYERS = (DEPTH + 1) // 2
N_ATTN_LAYERS = DEPTH // 2
LN_EPS = 1e-5

kernel_name = "hybrid_ssd_diffattn_dit_block"


def layer_norm(x, g, b):
    xf = x.astype(jnp.float32)
    mu = jnp.mean(xf, -1, keepdims=True)
    var = jnp.mean(jnp.square(xf - mu), -1, keepdims=True)
    return ((xf - mu) * lax.rsqrt(var + LN_EPS)).astype(x.dtype) * g + b


def rms_norm(x, w):
    xf = x.astype(jnp.float32)
    return (xf * lax.rsqrt(jnp.mean(jnp.square(xf), -1, keepdims=True) + LN_EPS)).astype(x.dtype) * w


def sq_relu_mlp(h, w_up, w_down):
    return jnp.square(jax.nn.relu(h @ w_up)) @ w_down


def centred_depthwise_conv(u, w, b):
    pad = (SSM_CONV - 1) // 2
    out = lax.conv_general_dilated(u, w[:, None, :], window_strides=(1,), padding=((pad, pad),),
                                   dimension_numbers=('NWC', 'WIO', 'NWC'),
                                   feature_group_count=u.shape[-1])
    return out + b


def ssd_chunked(x, dt, a, b_mat, c_mat, h0):
    f32 = jnp.float32
    bsz, L = x.shape[0], x.shape[1]
    nc = L // SSM_CHUNK
    hg = SSM_HEADS // SSM_GROUPS
    xc = x.astype(f32).reshape(bsz, nc, SSM_CHUNK, SSM_GROUPS, hg, SSM_HEAD_DIM)
    dtc = dt.astype(f32).reshape(bsz, nc, SSM_CHUNK, SSM_GROUPS, hg)
    bc = b_mat.astype(f32).reshape(bsz, nc, SSM_CHUNK, SSM_GROUPS, SSM_STATE)
    cc = c_mat.astype(f32).reshape(bsz, nc, SSM_CHUNK, SSM_GROUPS, SSM_STATE)
    cs = jnp.cumsum(dtc * a.astype(f32).reshape(SSM_GROUPS, hg), axis=2)
    lower = jnp.tril(jnp.ones((SSM_CHUNK, SSM_CHUNK), bool))[:, :, None, None]
    seg = cs[:, :, :, None] - cs[:, :, None, :]
    w = jnp.einsum('bcign,bcjgn->bcijg', cc, bc)[..., None] * jnp.exp(jnp.where(lower, seg, -jnp.inf)) * dtc[:, :, None]
    y_diag = jnp.einsum('bcijgh,bcjghp->bcighp', w, xc)
    xw = xc * (jnp.exp(cs[:, :, -1:] - cs) * dtc)[..., None]
    states = jnp.einsum('bcjgn,bcjghp->bcghpn', bc, xw)
    chunk_decay = jnp.exp(cs[:, :, -1])
    h0g = h0.astype(f32).reshape(bsz, SSM_GROUPS, hg, SSM_HEAD_DIM, SSM_STATE)

    def step(h, inp):
        s, d = inp
        return d[..., None, None] * h + s, h

    h_fin, h_start = lax.scan(step, h0g, (jnp.moveaxis(states, 1, 0), jnp.moveaxis(chunk_decay, 1, 0)))
    h_start = jnp.moveaxis(h_start, 0, 1)
    y_off = jnp.einsum('bcign,bcghpn->bcighp', cc, h_start) * jnp.exp(cs)[..., None]
    y = (y_diag + y_off).reshape(bsz, L, SSM_HEADS, SSM_HEAD_DIM)
    return y, h_fin.reshape(bsz, SSM_HEADS, SSM_HEAD_DIM, SSM_STATE)


def ssd_project(h, w_in, conv_w, conv_b, dt_bias):
    bsz, L = h.shape[0], h.shape[1]
    zxbcdt = h @ w_in
    z = zxbcdt[..., :D_INNER]
    xbc = zxbcdt[..., D_INNER:D_INNER + SSM_CONV_DIM]
    dt_raw = zxbcdt[..., D_INNER + SSM_CONV_DIM:].reshape(bsz, L, 2, SSM_HEADS)
    xbc = jax.nn.silu(centred_depthwise_conv(xbc, conv_w, conv_b))
    xs = xbc[..., :D_INNER].reshape(bsz, L, SSM_HEADS, SSM_HEAD_DIM)
    bcs = xbc[..., D_INNER:].reshape(bsz, L, 4, SSM_GROUPS, SSM_STATE)
    dt = jax.nn.softplus((dt_raw + dt_bias).astype(jnp.float32))
    return z, xs, bcs, dt


def ssd_bidir(xs, bcs, dt, a, d_skip, h0_fwd, h0_bwd):
    flip = lambda t: jnp.flip(t, axis=1)
    y_f, h_f = ssd_chunked(xs, dt[:, :, 0], a[0], bcs[:, :, 0], bcs[:, :, 1], h0_fwd)
    y_b, h_b = ssd_chunked(flip(xs), flip(dt[:, :, 1]), a[1], flip(bcs[:, :, 2]), flip(bcs[:, :, 3]), h0_bwd)
    y = y_f + flip(y_b) + (d_skip[0] + d_skip[1]).astype(jnp.float32)[:, None] * xs.astype(jnp.float32)
    return y, h_f, h_b


def ssd_output(y, z, norm_w, w_out):
    bsz, L = y.shape[0], y.shape[1]
    g = y.reshape(bsz, L, D_INNER) * jax.nn.silu(z.astype(jnp.float32))
    return rms_norm(g, norm_w).astype(z.dtype) @ w_out


def ssd_mixer(h_ctx, h_lat, w_in, conv_w, conv_b, dt_bias, a_log, d_skip, norm_w, w_out, ctx_out):
    a = -jnp.exp(a_log.astype(jnp.float32))
    zc, xc, bcc, dtc = ssd_project(h_ctx, w_in, conv_w, conv_b, dt_bias)
    h0 = jnp.zeros((h_ctx.shape[0], SSM_HEADS, SSM_HEAD_DIM, SSM_STATE), jnp.float32)
    yc, hc_f, hc_b = ssd_bidir(xc, bcc, dtc, a, d_skip, h0, h0)
    zl, xl, bcl, dtl = ssd_project(h_lat, w_in, conv_w, conv_b, dt_bias)
    yl, _, _ = ssd_bidir(xl, bcl, dtl, a, d_skip, hc_f, hc_b)
    y_lat = ssd_output(yl, zl, norm_w, w_out)
    y_ctx = ssd_output(yc, zc, norm_w, w_out) if ctx_out else None
    return y_ctx, y_lat


def axial_angles(L):
    rows = L // GRID_W
    t = jnp.arange(rows * GRID_W)
    row = (t // GRID_W).astype(jnp.float32)
    col = (t % GRID_W).astype(jnp.float32)
    half = ATTN_HEAD_DIM // 2
    inv_freq = ROPE_THETA ** (-jnp.arange(0, half, 2, dtype=jnp.float32) / half)
    return row[:, None] * inv_freq, col[:, None] * inv_freq


def rope_1d(x, ang):
    xf = x.astype(jnp.float32)
    cos = jnp.cos(ang)[None, :, None, :]
    sin = jnp.sin(ang)[None, :, None, :]
    h2 = x.shape[-1] // 2
    x1, x2 = xf[..., :h2], xf[..., h2:]
    return jnp.concatenate([x1 * cos - x2 * sin, x2 * cos + x1 * sin], axis=-1).astype(x.dtype)


def axial_rope(x, ang_r, ang_c):
    half = ATTN_HEAD_DIM // 2
    return jnp.concatenate([rope_1d(x[..., :half], ang_r), rope_1d(x[..., half:], ang_c)], axis=-1)


def diff_attend(q, k, v, lam):
    s = jnp.einsum('bqhd,bkhd->bhqk', q, k).astype(jnp.float32) * (ATTN_HEAD_DIM ** -0.5)
    p = jax.nn.softmax(s, axis=-1)
    p = p.reshape(p.shape[0], ATTN_HEADS, 2, p.shape[2], p.shape[3])
    attn = p[:, :, 0] - lam * p[:, :, 1]
    return jnp.einsum('bhqk,bkhd->bqhd', attn.astype(v.dtype), v)


def diff_attention_mixer(h_ctx, h_lat, w_qkv, lam_qk, subln_w, w_out, lambda_init, ctx_out):
    bsz, L = h_lat.shape[0], h_lat.shape[1]

    def split_qkv(h):
        n = h.shape[1]
        qkv = h @ w_qkv
        q = qkv[..., :D_MODEL].reshape(bsz, n, 2 * ATTN_HEADS, ATTN_HEAD_DIM)
        k = qkv[..., D_MODEL:2 * D_MODEL].reshape(bsz, n, 2 * ATTN_HEADS, ATTN_HEAD_DIM)
        v = qkv[..., 2 * D_MODEL:].reshape(bsz, n, ATTN_HEADS, 2 * ATTN_HEAD_DIM)
        return q, k, v

    q_l, k_l, v_l = split_qkv(h_lat)
    q_c, k_c, v_c = split_qkv(h_ctx)
    ang_r, ang_c = axial_angles(L)
    q_l = axial_rope(q_l, ang_r, ang_c)
    k_l = axial_rope(k_l, ang_r, ang_c)
    k_all = jnp.concatenate([k_l, k_c], axis=1)
    v_all = jnp.concatenate([v_l, v_c], axis=1)
    lf = lam_qk.astype(jnp.float32)
    lam = jnp.exp(jnp.sum(lf[0] * lf[1])) - jnp.exp(jnp.sum(lf[2] * lf[3])) + lambda_init
    n_blocks = L // ATTN_Q_BLOCK
    q_blocks = jnp.moveaxis(q_l.reshape(bsz, n_blocks, ATTN_Q_BLOCK, 2 * ATTN_HEADS, ATTN_HEAD_DIM), 1, 0)
    o_blocks = lax.map(lambda qb: diff_attend(qb, k_all, v_all, lam), q_blocks)
    o_l = jnp.moveaxis(o_blocks, 0, 1).reshape(bsz, L, ATTN_HEADS, 2 * ATTN_HEAD_DIM)

    def head_out(o):
        o = rms_norm(o, subln_w) * (1.0 - lambda_init)
        return o.reshape(bsz, o.shape[1], D_MODEL) @ w_out

    y_lat = head_out(o_l)
    y_ctx = head_out(diff_attend(q_c, k_c, v_c, lam)) if ctx_out else None
    return y_ctx, y_lat


def setup_inputs(seed: int = 0) -> dict:
    key = jax.random.key(seed)
    ks = jax.random.split(key, 32)
    D = D_MODEL

    def nrm(k, shape, s):
        return jax.random.normal(k, shape, jnp.float32) * s

    dt0 = jnp.exp(jax.random.uniform(ks[20], (N_SSM_LAYERS, 2, SSM_HEADS), jnp.float32,
                                     minval=math.log(DT_MIN), maxval=math.log(DT_MAX)))
    return {
        'x': nrm(ks[0], (BATCH, SEQ, D), 1.0),
        'c': nrm(ks[1], (BATCH, D), 1.0),
        'ctx': nrm(ks[2], (BATCH, CTX_LEN, D), 1.0),
        'c_ctx': nrm(ks[3], (D,), 1.0),
        'ada_w': nrm(ks[4], (DEPTH, D, 6 * D), 0.5 * D ** -0.5),
        'ada_b': nrm(ks[5], (DEPTH, 6 * D), 0.01),
        'ln_mix_g': 1.0 + nrm(ks[6], (DEPTH, D), 0.05),
        'ln_mix_b': nrm(ks[7], (DEPTH, D), 0.01),
        'mlp_w_up': nrm(ks[8], (DEPTH, D, MLP_HIDDEN), D ** -0.5),
        'mlp_w_down': nrm(ks[9], (DEPTH, MLP_HIDDEN, D), DEEPNORM_BETA * MLP_HIDDEN ** -0.5),
        'ln_mlp_g': 1.0 + nrm(ks[10], (DEPTH, D), 0.05),
        'ln_mlp_b': nrm(ks[11], (DEPTH, D), 0.01),
        'ssm_w_in': nrm(ks[12], (N_SSM_LAYERS, D, SSM_IN_DIM), D ** -0.5),
        'ssm_conv_w': nrm(ks[13], (N_SSM_LAYERS, SSM_CONV, SSM_CONV_DIM), SSM_CONV ** -0.5),
        'ssm_conv_b': nrm(ks[14], (N_SSM_LAYERS, SSM_CONV_DIM), 0.01),
        'ssm_dt_bias': dt0 + jnp.log(-jnp.expm1(-dt0)),
        'ssm_a_log': jnp.log(jax.random.uniform(ks[15], (N_SSM_LAYERS, 2, SSM_HEADS), jnp.float32, minval=1.0, maxval=16.0)),
        'ssm_d': 1.0 + nrm(ks[16], (N_SSM_LAYERS, 2, SSM_HEADS), 0.05),
        'ssm_norm_w': 1.0 + nrm(ks[17], (N_SSM_LAYERS, D_INNER), 0.05),
        'ssm_w_out': nrm(ks[18], (N_SSM_LAYERS, D_INNER, D), DEEPNORM_BETA * D_INNER ** -0.5),
        'attn_w_qkv': nrm(ks[19], (N_ATTN_LAYERS, D, 3 * D), D ** -0.5),
        'attn_lambda': nrm(ks[21], (N_ATTN_LAYERS, 4, ATTN_HEAD_DIM), 0.1),
        'attn_subln_w': 1.0 + nrm(ks[22], (N_ATTN_LAYERS, 2 * ATTN_HEAD_DIM), 0.05),
        'attn_w_out': nrm(ks[23], (N_ATTN_LAYERS, D, D), DEEPNORM_BETA * D ** -0.5),
    }


def reference(x, c, ctx, c_ctx, ada_w, ada_b, ln_mix_g, ln_mix_b, mlp_w_up, mlp_w_down,
              ln_mlp_g, ln_mlp_b, ssm_w_in, ssm_conv_w, ssm_conv_b, ssm_dt_bias, ssm_a_log,
              ssm_d, ssm_norm_w, ssm_w_out, attn_w_qkv, attn_lambda, attn_subln_w, attn_w_out):
    silu_c = jax.nn.silu(c)
    silu_cc = jax.nn.silu(c_ctx)
    for i in range(DEPTH):
        last = i == DEPTH - 1
        mod = jnp.split(silu_c @ ada_w[i] + ada_b[i], 6, axis=-1)
        sh1, sc1, g1, sh2, sc2, g2 = [m[:, None, :] for m in mod]
        sh1c, sc1c, g1c, sh2c, sc2c, g2c = jnp.split(silu_cc @ ada_w[i] + ada_b[i], 6, axis=-1)
        h_lat = x * (1.0 + sc1) + sh1
        h_ctx = ctx * (1.0 + sc1c) + sh1c
        j = i // N_MIXERS
        if i % N_MIXERS == 0:
            y_ctx, y_lat = ssd_mixer(h_ctx, h_lat, ssm_w_in[j], ssm_conv_w[j], ssm_conv_b[j],
                                     ssm_dt_bias[j], ssm_a_log[j], ssm_d[j], ssm_norm_w[j],
                                     ssm_w_out[j], not last)
        else:
            lambda_init = 0.8 - 0.6 * math.exp(-0.3 * i)
            y_ctx, y_lat = diff_attention_mixer(h_ctx, h_lat, attn_w_qkv[j], attn_lambda[j],
                                                attn_subln_w[j], attn_w_out[j], lambda_init, not last)
        x = layer_norm(DEEPNORM_ALPHA * x + g1 * y_lat, ln_mix_g[i], ln_mix_b[i])
        x = layer_norm(DEEPNORM_ALPHA * x + g2 * sq_relu_mlp(x * (1.0 + sc2) + sh2, mlp_w_up[i], mlp_w_down[i]),
                       ln_mlp_g[i], ln_mlp_b[i])
        if not last:
            ctx = layer_norm(DEEPNORM_ALPHA * ctx + g1c * y_ctx, ln_mix_g[i], ln_mix_b[i])
            ctx = layer_norm(DEEPNORM_ALPHA * ctx + g2c * sq_relu_mlp(ctx * (1.0 + sc2c) + sh2c, mlp_w_up[i], mlp_w_down[i]),
                             ln_mlp_g[i], ln_mlp_b[i])
    return x
```

```python
import functools
import math

import jax
import jax.numpy as jnp
from jax import lax
from jax.experimental import pallas as pl
from jax.experimental.pallas import tpu as pltpu

F32 = jnp.float32
BF16 = jnp.bfloat16

SSM_GROUPS = 8
SSM_STATE = 128
SSM_CHUNK = 128
SSM_CONV = 5
GRID_W = 64
ROPE_THETA = 10000.0
LN_EPS = 1e-5
N_MIXERS = 2

V7X_VMEM_LIMIT_BYTES = 56 * 1024 * 1024
SUBLANES = 8


def _params(*sem):
    return pltpu.CompilerParams(dimension_semantics=sem, vmem_limit_bytes=V7X_VMEM_LIMIT_BYTES)


def _pick(total, pref):
    t = min(pref, total)
    while total % t:
        t //= 2
    return t


def _ada_kernel(c_ref, w_ref, b_ref, o_ref):
    c = c_ref[...]
    s = (c * jax.nn.sigmoid(c)).astype(BF16)
    o_ref[...] = jnp.dot(s, w_ref[...].astype(BF16), preferred_element_type=F32) + b_ref[...]


def _ada_modulation(c8, ada_w, ada_b):
    depth, d, n = ada_w.shape
    tn = _pick(n, 1024)
    return pl.pallas_call(
        _ada_kernel,
        grid=(depth, n // tn),
        in_specs=[
            pl.BlockSpec((SUBLANES, d), lambda l, j: (0, 0)),
            pl.BlockSpec((None, d, tn), lambda l, j: (l, 0, j)),
            pl.BlockSpec((None, 1, tn), lambda l, j: (l, 0, j)),
        ],
        out_specs=pl.BlockSpec((None, SUBLANES, tn), lambda l, j: (l, 0, j)),
        out_shape=jax.ShapeDtypeStruct((depth, SUBLANES, n), F32),
        compiler_params=_params("arbitrary", "arbitrary"),
        name="ada_modulation",
    )(c8, ada_w, ada_b.reshape(depth, 1, n))


def _modulate_into(h_ref, x_ref, sc_ref, sh_ref):
    h_ref[...] = (x_ref[...] * (1.0 + sc_ref[...]) + sh_ref[...]).astype(BF16)


def _mm_mod_kernel(x_ref, sc_ref, sh_ref, w_ref, o_ref, h_ref, *, act):
    @pl.when(pl.program_id(2) == 0)
    def _():
        _modulate_into(h_ref, x_ref, sc_ref, sh_ref)

    acc = jnp.dot(h_ref[...], w_ref[...], preferred_element_type=F32)
    if act == "relu2":
        acc = jnp.square(jnp.maximum(acc, 0.0))
    o_ref[...] = acc.astype(o_ref.dtype)


def _mod_matmul(x3, sc, sh, w, *, act, out_dtype, tm=1024, tn=1024):
    g_, r, d = x3.shape
    n = w.shape[1]
    tm, tn = _pick(r, tm), _pick(n, tn)
    ni = r // tm
    return pl.pallas_call(
        functools.partial(_mm_mod_kernel, act=act),
        grid=(g_, ni, n // tn),
        in_specs=[
            pl.BlockSpec((None, tm, d), lambda g, i, j: (g, i, 0)),
            pl.BlockSpec((None, 1, d), lambda g, i, j: (g, 0, 0)),
            pl.BlockSpec((None, 1, d), lambda g, i, j: (g, 0, 0)),
            pl.BlockSpec((d, tn), lambda g, i, j: (0, j)),
        ],
        out_specs=pl.BlockSpec((tm, tn), lambda g, i, j: (g * ni + i, j)),
        out_shape=jax.ShapeDtypeStruct((g_ * r, n), out_dtype),
        scratch_shapes=[pltpu.VMEM((tm, d), BF16)],
        compiler_params=_params("arbitrary", "arbitrary", "arbitrary"),
        name="mod_matmul_" + act,
    )(x3, sc, sh, w)


def _softplus(v):
    return jnp.maximum(v, 0.0) + jnp.log1p(jnp.exp(-jnp.abs(v)))


def _in_proj_kernel(x_ref, sc_ref, sh_ref, w_ref, wdt_ref, dtb_ref, o_ref, odt_ref, h_ref):
    @pl.when(pl.program_id(2) == 0)
    def _():
        _modulate_into(h_ref, x_ref, sc_ref, sh_ref)
        dt_raw = jnp.dot(h_ref[...], wdt_ref[...], preferred_element_type=F32)
        odt_ref[...] = _softplus(dt_raw + dtb_ref[...])

    o_ref[...] = jnp.dot(h_ref[...], w_ref[...], preferred_element_type=F32)


def _ssd_in_proj(x3, sc, sh, w_zx, w_dt, dt_bias, *, tm=1024, tn=1024):
    g_, r, d = x3.shape
    n = w_zx.shape[1]
    ndt = w_dt.shape[1]
    tm, tn = _pick(r, tm), _pick(n, tn)
    ni = r // tm
    return pl.pallas_call(
        _in_proj_kernel,
        grid=(g_, ni, n // tn),
        in_specs=[
            pl.BlockSpec((None, tm, d), lambda g, i, j: (g, i, 0)),
            pl.BlockSpec((None, 1, d), lambda g, i, j: (g, 0, 0)),
            pl.BlockSpec((None, 1, d), lambda g, i, j: (g, 0, 0)),
            pl.BlockSpec((d, tn), lambda g, i, j: (0, j)),
            pl.BlockSpec((d, ndt), lambda g, i, j: (0, 0)),
            pl.BlockSpec((1, ndt), lambda g, i, j: (0, 0)),
        ],
        out_specs=[
            pl.BlockSpec((tm, tn), lambda g, i, j: (g * ni + i, j)),
            pl.BlockSpec((tm, ndt), lambda g, i, j: (g * ni + i, 0)),
        ],
        out_shape=[
            jax.ShapeDtypeStruct((g_ * r, n), F32),
            jax.ShapeDtypeStruct((g_ * r, ndt), F32),
        ],
        scratch_shapes=[pltpu.VMEM((tm, d), BF16)],
        compiler_params=_params("arbitrary", "arbitrary", "arbitrary"),
        name="ssd_in_proj",
    )(x3, sc, sh, w_zx, w_dt, dt_bias)


def _rope(v, cos, sin, first_half):
    dh = v.shape[1]
    partner = jnp.where(first_half, pltpu.roll(v, dh - dh // 4, 1), pltpu.roll(v, dh // 4, 1))
    return v * cos + partner * sin


def _qkv_kernel(x_ref, sc_ref, sh_ref, w_ref, cos_ref, sin_ref, o_ref, h_ref, *, nq, nk, dh, scale):
    j = pl.program_id(2)

    @pl.when(j == 0)
    def _():
        _modulate_into(h_ref, x_ref, sc_ref, sh_ref)

    acc = jnp.dot(h_ref[...], w_ref[...], preferred_element_type=F32)
    tn = acc.shape[1]

    def roped(mult):
        cos, sin = cos_ref[...], sin_ref[...]
        lane = lax.broadcasted_iota(jnp.int32, cos.shape, 1)
        first_half = (lane % (dh // 2)) < (dh // 4)
        parts = []
        for k in range(tn // dh):
            r = _rope(acc[:, k * dh:(k + 1) * dh], cos, sin, first_half)
            parts.append(r * mult if mult is not None else r)
        return jnp.concatenate(parts, axis=1)

    @pl.when(j < nq)
    def _():
        o_ref[...] = roped(scale).astype(o_ref.dtype)

    @pl.when((j >= nq) & (j < nq + nk))
    def _():
        o_ref[...] = roped(None).astype(o_ref.dtype)

    @pl.when(j >= nq + nk)
    def _():
        o_ref[...] = acc.astype(o_ref.dtype)


def _qkv_proj(x3, sc, sh, w, cos, sin, *, dh, tm=1024, tn=1024):
    g_, r, d = x3.shape
    n = w.shape[1]
    tm, tn = _pick(r, tm), _pick(d, tn)
    ni = r // tm
    return pl.pallas_call(
        functools.partial(_qkv_kernel, nq=d // tn, nk=d // tn, dh=dh, scale=dh ** -0.5),
        grid=(g_, ni, n // tn),
        in_specs=[
            pl.BlockSpec((None, tm, d), lambda g, i, j: (g, i, 0)),
            pl.BlockSpec((None, 1, d), lambda g, i, j: (g, 0, 0)),
            pl.BlockSpec((None, 1, d), lambda g, i, j: (g, 0, 0)),
            pl.BlockSpec((d, tn), lambda g, i, j: (0, j)),
            pl.BlockSpec((tm, dh), lambda g, i, j: (i, 0)),
            pl.BlockSpec((tm, dh), lambda g, i, j: (i, 0)),
        ],
        out_specs=pl.BlockSpec((tm, tn), lambda g, i, j: (g * ni + i, j)),
        out_shape=jax.ShapeDtypeStruct((g_ * r, n), BF16),
        scratch_shapes=[pltpu.VMEM((tm, d), BF16)],
        compiler_params=_params("arbitrary", "arbitrary", "arbitrary"),
        name="qkv_proj",
    )(x3, sc, sh, w, cos, sin)


def _res_ln(acc, x_ref, g_ref, lng_ref, lnb_ref, o_ref, alpha):
    r = alpha * x_ref[...] + g_ref[...] * acc
    mu = jnp.mean(r, axis=-1, keepdims=True)
    rc = r - mu
    var = jnp.mean(jnp.square(rc), axis=-1, keepdims=True)
    o_ref[...] = (rc * lax.rsqrt(var + LN_EPS)) * lng_ref[...] + lnb_ref[...]


def _mm_res_ln_kernel(a_ref, w_ref, x_ref, g_ref, lng_ref, lnb_ref, o_ref, acc_ref, *, nk, alpha):
    k = pl.program_id(2)
    part = jnp.dot(a_ref[...], w_ref[...], preferred_element_type=F32)

    @pl.when(k == 0)
    def _():
        acc_ref[...] = part

    @pl.when(k > 0)
    def _():
        acc_ref[...] += part

    @pl.when(k == nk - 1)
    def _():
        _res_ln(acc_ref[...], x_ref, g_ref, lng_ref, lnb_ref, o_ref, alpha)


def _matmul_res_ln(a, w, x3, gate, ln_g, ln_b, *, alpha, tm=512, tk=1024):
    g_, r, d = x3.shape
    kdim = a.shape[1]
    tm, tk = _pick(r, tm), _pick(kdim, tk)
    ni, nk = r // tm, kdim // tk
    return pl.pallas_call(
        functools.partial(_mm_res_ln_kernel, nk=nk, alpha=alpha),
        grid=(g_, ni, nk),
        in_specs=[
            pl.BlockSpec((tm, tk), lambda g, i, k: (g * ni + i, k)),
            pl.BlockSpec((tk, d), lambda g, i, k: (k, 0)),
            pl.BlockSpec((None, tm, d), lambda g, i, k: (g, i, 0)),
            pl.BlockSpec((None, 1, d), lambda g, i, k: (g, 0, 0)),
            pl.BlockSpec((1, d), lambda g, i, k: (0, 0)),
            pl.BlockSpec((1, d), lambda g, i, k: (0, 0)),
        ],
        out_specs=pl.BlockSpec((None, tm, d), lambda g, i, k: (g, i, 0)),
        out_shape=jax.ShapeDtypeStruct((g_, r, d), F32),
        scratch_shapes=[pltpu.VMEM((tm, d), F32)],
        compiler_params=_params("arbitrary", "arbitrary", "arbitrary"),
        name="matmul_res_ln",
    )(a, w, x3, gate, ln_g.reshape(1, d), ln_b.reshape(1, d))


def _conv_silu_kernel(main_ref, prev_ref, next_ref, w_ref, b_ref, o_ref, s_ref, *, tiles_per_seq):
    t = pl.program_id(1) % tiles_per_seq
    tl = main_ref.shape[0]
    halo = SUBLANES
    pad = (SSM_CONV - 1) // 2
    s_ref[0:halo, :] = jnp.where(t > 0, prev_ref[...], 0.0)
    s_ref[halo:halo + tl, :] = main_ref[...]
    s_ref[halo + tl:2 * halo + tl, :] = jnp.where(t < tiles_per_seq - 1, next_ref[...], 0.0)
    acc = jnp.broadcast_to(b_ref[...], o_ref.shape)
    for k in range(SSM_CONV):
        off = halo - pad + k
        acc = acc + s_ref[off:off + tl, :] * w_ref[k:k + 1, :]
    o_ref[...] = (acc * jax.nn.sigmoid(acc)).astype(o_ref.dtype)


def _conv_silu(zx, conv_w, conv_b, *, col0, seq_len, tl=512, tc=512):
    m = zx.shape[0]
    cdim = conv_w.shape[1]
    tl = _pick(seq_len, tl)
    tps = seq_len // tl
    cb0 = col0 // tc
    hb = tl // SUBLANES
    last_hb = m // SUBLANES - 1
    return pl.pallas_call(
        functools.partial(_conv_silu_kernel, tiles_per_seq=tps),
        grid=(cdim // tc, m // tl),
        in_specs=[
            pl.BlockSpec((tl, tc), lambda cb, i: (i, cb0 + cb)),
            pl.BlockSpec((SUBLANES, tc), lambda cb, i: (jnp.maximum(i * hb - 1, 0), cb0 + cb)),
            pl.BlockSpec((SUBLANES, tc), lambda cb, i: (jnp.minimum((i + 1) * hb, last_hb), cb0 + cb)),
            pl.BlockSpec((SSM_CONV, tc), lambda cb, i: (0, cb)),
            pl.BlockSpec((1, tc), lambda cb, i: (0, cb)),
        ],
        out_specs=pl.BlockSpec((tl, tc), lambda cb, i: (i, cb)),
        out_shape=jax.ShapeDtypeStruct((m, cdim), BF16),
        scratch_shapes=[pltpu.VMEM((tl + 2 * SUBLANES, tc), F32)],
        compiler_params=_params("arbitrary", "arbitrary"),
        name="conv_silu",
    )(zx, zx, zx, conv_w, conv_b.reshape(1, cdim))


def _split3(v):
    a1 = v.astype(BF16)
    r1 = v - a1.astype(F32)
    a2 = r1.astype(BF16)
    a3 = (r1 - a2.astype(F32)).astype(BF16)
    return a1, a2, a3


def _ssd_direction(x_ref, b_ref, c_ref, dt_ref, a_col, s_ref, y_ref, *, reverse, hg, p):
    q = SSM_CHUNK
    ii = lax.broadcasted_iota(jnp.int32, (q, q), 0)
    jj = lax.broadcasted_iota(jnp.int32, (q, q), 1)
    causal = (ii <= jj) if reverse else (ii >= jj)
    tri_t = ((ii >= jj) if reverse else (ii <= jj)).astype(BF16)

    x = x_ref[...]
    bm = b_ref[...]
    cm = c_ref[...]
    dt_row = dt_ref[...]
    da_row = dt_row * a_col
    cs_row = sum(jnp.dot(t, tri_t, preferred_element_type=F32) for t in _split3(da_row))
    stacked = jnp.concatenate([cs_row, dt_row, jnp.zeros((q - 2 * hg, q), F32)], axis=0)
    cols = stacked.T

    cb = lax.dot_general(cm, bm, (((1,), (1,)), ((), ())), preferred_element_type=F32)
    c_state = jnp.dot(cm, s_ref[...].astype(BF16), preferred_element_type=F32)
    end = 0 if reverse else q - 1

    for h in range(hg):
        lo, hi = h * p, (h + 1) * p
        cs_c = cols[:, h:h + 1]
        dt_c = cols[:, hg + h:hg + h + 1]
        cs_r = cs_row[h:h + 1, :]
        dt_r = dt_row[h:h + 1, :]
        decay = jnp.exp(jnp.where(causal, cs_c - cs_r, -jnp.inf)) * dt_r
        w = (cb * decay).astype(BF16)
        xh = x[:, lo:hi]
        y_ref[:, lo:hi] = (jnp.dot(w, xh, preferred_element_type=F32)
                           + c_state[:, lo:hi] * jnp.exp(cs_c))
        cs_end = cs_c[end:end + 1, :]
        xw = (xh.astype(F32) * (jnp.exp(cs_end - cs_c) * dt_c)).astype(BF16)
        st = lax.dot_general(bm, xw, (((0,), (0,)), ((), ())), preferred_element_type=F32)
        s_ref[:, lo:hi] = jnp.exp(cs_end) * s_ref[:, lo:hi] + st


def _ssd_kernel(xf_ref, xb_ref, bf_ref, cf_ref, bb_ref, cb_ref, dtf_ref, dtb_ref, alog_ref,
                h0f_ref, h0b_ref, yf_ref, yb_ref, hff_ref, hfb_ref, sf_ref, sb_ref, *, nc, hg, p):
    c = pl.program_id(2)

    @pl.when(c == 0)
    def _():
        sf_ref[...] = h0f_ref[...]
        sb_ref[...] = h0b_ref[...]

    a = -jnp.exp(alog_ref[...])
    _ssd_direction(xf_ref, bf_ref, cf_ref, dtf_ref, a[0], sf_ref, yf_ref, reverse=False, hg=hg, p=p)
    _ssd_direction(xb_ref, bb_ref, cb_ref, dtb_ref, a[1], sb_ref, yb_ref, reverse=True, hg=hg, p=p)

    @pl.when(c == nc - 1)
    def _():
        hff_ref[...] = sf_ref[...]
        hfb_ref[...] = sb_ref[...]


def _ssd_scan(xbc, dt_rows, a_log, h0f, h0b, *, n_seq, seq_len, d_inner, p):
    m = xbc.shape[0]
    g_, n, q = SSM_GROUPS, SSM_STATE, SSM_CHUNK
    gw = d_inner // g_
    hg = gw // p
    nc = seq_len // q
    nb0 = d_inner // n

    def row_f(b, g, c):
        return b * nc + c

    def row_b(b, g, c):
        return b * nc + nc - 1 - c

    def xspec(row):
        return pl.BlockSpec((q, gw), lambda b, g, c: (row(b, g, c), g))

    def bcspec(row, k):
        return pl.BlockSpec((q, n), lambda b, g, c: (row(b, g, c), nb0 + k * g_ + g))

    def dtspec(row, d):
        return pl.BlockSpec((None, None, hg, q), lambda b, g, c: (d, g, 0, row(b, g, c)))

    state_spec = pl.BlockSpec((None, None, n, gw), lambda b, g, c: (b, g, 0, 0))
    state_shape = jax.ShapeDtypeStruct((n_seq, g_, n, gw), F32)
    return pl.pallas_call(
        functools.partial(_ssd_kernel, nc=nc, hg=hg, p=p),
        grid=(n_seq, g_, nc),
        in_specs=[
            xspec(row_f), xspec(row_b),
            bcspec(row_f, 0), bcspec(row_f, 1), bcspec(row_b, 2), bcspec(row_b, 3),
            dtspec(row_f, 0), dtspec(row_b, 1),
            pl.BlockSpec((2, None, hg, 1), lambda b, g, c: (0, g, 0, 0)),
            state_spec, state_spec,
        ],
        out_specs=[xspec(row_f), xspec(row_b), state_spec, state_spec],
        out_shape=[
            jax.ShapeDtypeStruct((m, d_inner), F32),
            jax.ShapeDtypeStruct((m, d_inner), F32),
            state_shape, state_shape,
        ],
        scratch_shapes=[pltpu.VMEM((n, gw), F32), pltpu.VMEM((n, gw), F32)],
        compiler_params=_params("arbitrary", "arbitrary", "arbitrary"),
        name="ssd_scan",
    )(xbc, xbc, xbc, xbc, xbc, xbc, dt_rows, dt_rows, a_log, h0f, h0b)


def _ssd_gate_kernel(yf_ref, yb_ref, x_ref, z_ref, d_ref, nw_ref, o_ref):
    d = d_ref[0:1, :] + d_ref[1:2, :]
    y = yf_ref[...] + yb_ref[...] + d * x_ref[...].astype(F32)
    z = z_ref[...]
    g = y * (z * jax.nn.sigmoid(z))
    ms = jnp.mean(jnp.square(g), axis=-1, keepdims=True)
    o_ref[...] = ((g * lax.rsqrt(ms + LN_EPS)) * nw_ref[...]).astype(o_ref.dtype)


def _ssd_gate(yf, yb, xbc, zx, d_rep, norm_w, *, d_inner, tm=128):
    m = yf.shape[0]
    tm = _pick(m, tm)
    row = lambda i: (i, 0)
    return pl.pallas_call(
        _ssd_gate_kernel,
        grid=(m // tm,),
        in_specs=[
            pl.BlockSpec((tm, d_inner), row),
            pl.BlockSpec((tm, d_inner), row),
            pl.BlockSpec((tm, d_inner), row),
            pl.BlockSpec((tm, d_inner), row),
            pl.BlockSpec((2, d_inner), lambda i: (0, 0)),
            pl.BlockSpec((1, d_inner), lambda i: (0, 0)),
        ],
        out_specs=pl.BlockSpec((tm, d_inner), row),
        out_shape=jax.ShapeDtypeStruct((m, d_inner), BF16),
        compiler_params=_params("arbitrary"),
        name="ssd_gate_norm",
    )(yf, yb, xbc, zx, d_rep, norm_w.reshape(1, d_inner))


def _attn_kernel(q_ref, kl_ref, vl_ref, kc_ref, vc_ref, lam_ref, sw_ref, o_ref, *, dh, lambda_init):
    nt = (((1,), (1,)), ((), ()))
    lf = lam_ref[...]
    lam = (jnp.exp(jnp.sum(lf[0:1] * lf[1:2], axis=-1, keepdims=True))
           - jnp.exp(jnp.sum(lf[2:3] * lf[3:4], axis=-1, keepdims=True)) + lambda_init)

    def softmax_parts(lo):
        qs = q_ref[:, lo:lo + dh]
        sl = lax.dot_general(qs, kl_ref[:, lo:lo + dh], nt, preferred_element_type=F32)
        sc = lax.dot_general(qs, kc_ref[:, lo:lo + dh], nt, preferred_element_type=F32)
        mx = jnp.maximum(jnp.max(sl, axis=-1, keepdims=True), jnp.max(sc, axis=-1, keepdims=True))
        el, ec = jnp.exp(sl - mx), jnp.exp(sc - mx)
        inv = 1.0 / (jnp.sum(el, axis=-1, keepdims=True) + jnp.sum(ec, axis=-1, keepdims=True))
        return el, ec, inv

    e1l, e1c, i1 = softmax_parts(0)
    e2l, e2c, i2 = softmax_parts(dh)
    i2 = lam * i2
    pl_ = (e1l * i1 - e2l * i2).astype(BF16)
    pc_ = (e1c * i1 - e2c * i2).astype(BF16)
    o = (jnp.dot(pl_, vl_ref[...], preferred_element_type=F32)
         + jnp.dot(pc_, vc_ref[...], preferred_element_type=F32))
    ms = jnp.mean(jnp.square(o), axis=-1, keepdims=True)
    o = (o * lax.rsqrt(ms + LN_EPS)) * sw_ref[...] * (1.0 - lambda_init)
    o_ref[...] = o.astype(o_ref.dtype)


def _diff_attention(qkv, kv_ctx, lam_qk, subln_w, *, n_seq, seq_len, ctx_len, d, dh, lambda_init, tq=256):
    hw = 2 * dh
    nh = d // hw
    tq = _pick(seq_len, tq)
    nq = seq_len // tq
    return pl.pallas_call(
        functools.partial(_attn_kernel, dh=dh, lambda_init=lambda_init),
        grid=(n_seq, nh, nq),
        in_specs=[
            pl.BlockSpec((tq, hw), lambda b, h, i: (b * nq + i, h)),
            pl.BlockSpec((seq_len, hw), lambda b, h, i: (b, nh + h)),
            pl.BlockSpec((seq_len, hw), lambda b, h, i: (b, 2 * nh + h)),
            pl.BlockSpec((ctx_len, hw), lambda b, h, i: (b, h)),
            pl.BlockSpec((ctx_len, hw), lambda b, h, i: (b, nh + h)),
            pl.BlockSpec((4, dh), lambda b, h, i: (0, 0)),
            pl.BlockSpec((1, hw), lambda b, h, i: (0, 0)),
        ],
        out_specs=pl.BlockSpec((tq, hw), lambda b, h, i: (b * nq + i, h)),
        out_shape=jax.ShapeDtypeStruct((n_seq * seq_len, d), BF16),
        compiler_params=_params("arbitrary", "arbitrary", "arbitrary"),
        name="diff_attention",
    )(qkv, qkv, qkv, kv_ctx, kv_ctx, lam_qk, subln_w.reshape(1, hw))


def _rope_tables(seq_len, dh):
    t = jnp.arange(seq_len)
    row = (t // GRID_W).astype(F32)
    col = (t % GRID_W).astype(F32)
    half = dh // 2
    inv_freq = ROPE_THETA ** (-jnp.arange(0, half, 2, dtype=F32) / half)
    ang_r, ang_c = row[:, None] * inv_freq, col[:, None] * inv_freq
    cos = jnp.concatenate([jnp.cos(ang_r)] * 2 + [jnp.cos(ang_c)] * 2, axis=-1)
    sin = jnp.concatenate([-jnp.sin(ang_r), jnp.sin(ang_r), -jnp.sin(ang_c), jnp.sin(ang_c)], axis=-1)
    return cos, sin


def _ssd_mixer(streams, w_in, conv_w, conv_b, dt_bias, a_log, d_skip, norm_w, *, p=64):
    d_inner = norm_w.shape[0]
    cdim = conv_w.shape[1]
    n_heads = a_log.shape[1]
    g_ = SSM_GROUPS
    hg = n_heads // g_
    gw = d_inner // g_
    w_zx = w_in[:, :d_inner + cdim].astype(BF16)
    w_dt = w_in[:, d_inner + cdim:].astype(BF16)
    dtb = dt_bias.reshape(1, 2 * n_heads)
    alog = a_log.reshape(2, g_, hg, 1)
    d_rep = jnp.repeat(d_skip, p, axis=1)
    outs = []
    h0f = h0b = None
    for x3, sc, sh, n_seq, seq_len in streams:
        zx, dt = _ssd_in_proj(x3, sc, sh, w_zx, w_dt, dtb)
        xbc = _conv_silu(zx, conv_w, conv_b, col0=d_inner, seq_len=seq_len)
        dt_rows = dt.reshape(-1, 2, g_, hg).transpose(1, 2, 3, 0)
        if h0f is None:
            h0f = h0b = jnp.zeros((n_seq, g_, SSM_STATE, gw), F32)
        yf, yb, h0f, h0b = _ssd_scan(xbc, dt_rows, alog, h0f, h0b, n_seq=n_seq, seq_len=seq_len,
                                     d_inner=d_inner, p=p)
        outs.append(_ssd_gate(yf, yb, xbc, zx, d_rep, norm_w, d_inner=d_inner))
    return outs


def kernel(x, c, ctx, c_ctx, ada_w, ada_b, ln_mix_g, ln_mix_b, mlp_w_up, mlp_w_down, ln_mlp_g, ln_mlp_b, ssm_w_in, ssm_conv_w, ssm_conv_b, ssm_dt_bias, ssm_a_log, ssm_d, ssm_norm_w, ssm_w_out, attn_w_qkv, attn_lambda, attn_subln_w, attn_w_out):
    bsz, seq, d = x.shape
    ctx_len = ctx.shape[1]
    depth = ada_w.shape[0]
    alpha = (2 * depth) ** 0.25
    dh = attn_lambda.shape[-1]
    assert bsz + 1 <= SUBLANES

    c8 = jnp.concatenate([c, c_ctx[None, :], jnp.zeros((SUBLANES - bsz - 1, d), F32)], axis=0)
    mod = _ada_modulation(c8, ada_w, ada_b).reshape(depth, SUBLANES, 6, d)

    lat = x
    cx = ctx.reshape(1, bsz * ctx_len, d)
    for i in range(depth):
        last = i == depth - 1
        sh1, sc1, g1, sh2, sc2, g2 = [mod[i, :bsz, k][:, None, :] for k in range(6)]
        sh1c, sc1c, g1c, sh2c, sc2c, g2c = [mod[i, bsz:bsz + 1, k][:, None, :] for k in range(6)]
        j = i // N_MIXERS
        if i % N_MIXERS == 0:
            streams = [(cx, sc1c, sh1c, bsz, ctx_len), (lat, sc1, sh1, bsz, seq)]
            y_ctx, y_lat = _ssd_mixer(streams, ssm_w_in[j], ssm_conv_w[j], ssm_conv_b[j], ssm_dt_bias[j],
                                      ssm_a_log[j], ssm_d[j], ssm_norm_w[j])
            w_out = ssm_w_out[j].astype(BF16)
        else:
            if not last:
                raise NotImplementedError("context queries are only needed when attention is not the last layer")
            lambda_init = 0.8 - 0.6 * math.exp(-0.3 * i)
            w_qkv = attn_w_qkv[j].astype(BF16)
            cos, sin = _rope_tables(seq, dh)
            qkv = _qkv_proj(lat, sc1, sh1, w_qkv, cos, sin, dh=dh)
            kv_ctx = _mod_matmul(cx, sc1c, sh1c, w_qkv[:, d:], act="none", out_dtype=BF16)
            y_lat = _diff_attention(qkv, kv_ctx, attn_lambda[j], attn_subln_w[j], n_seq=bsz, seq_len=seq,
                                    ctx_len=ctx_len, d=d, dh=dh, lambda_init=lambda_init)
            y_ctx = None
            w_out = attn_w_out[j].astype(BF16)
        w_up = mlp_w_up[i].astype(BF16)
        w_down = mlp_w_down[i].astype(BF16)

        def finish(x3, y, gate1, sc, sh, gate2):
            x3 = _matmul_res_ln(y, w_out, x3, gate1, ln_mix_g[i], ln_mix_b[i], alpha=alpha)
            u = _mod_matmul(x3, sc, sh, w_up, act="relu2", out_dtype=BF16)
            return _matmul_res_ln(u, w_down, x3, gate2, ln_mlp_g[i], ln_mlp_b[i], alpha=alpha)

        lat = finish(lat, y_lat, g1, sc2, sh2, g2)
        if not last:
            cx = finish(cx, y_ctx, g1c, sc2c, sh2c, g2c)
    return lat
```

```python
import functools
import math

import jax
import jax.numpy as jnp
from jax import lax
from jax.experimental import pallas as pl
from jax.experimental.pallas import tpu as pltpu

F32 = jnp.float32
BF16 = jnp.bfloat16

SSM_GROUPS = 8
SSM_STATE = 128
SSM_CHUNK = 128
SSM_CONV = 5
GRID_W = 64
ROPE_THETA = 10000.0
LN_EPS = 1e-5
N_MIXERS = 2

V7X_VMEM_LIMIT_BYTES = 56 * 1024 * 1024
SUBLANES = 8


def _params(*sem):
    return pltpu.CompilerParams(dimension_semantics=sem, vmem_limit_bytes=V7X_VMEM_LIMIT_BYTES)


def _pick(total, pref):
    t = min(pref, total)
    while total % t:
        t //= 2
    return t


def _ada_kernel(c_ref, w_ref, b_ref, o_ref):
    c = c_ref[...]
    s = (c * jax.nn.sigmoid(c)).astype(BF16)
    o_ref[...] = jnp.dot(s, w_ref[...].astype(BF16), preferred_element_type=F32) + b_ref[...]


def _ada_modulation(c8, ada_w, ada_b):
    depth, d, n = ada_w.shape
    tn = _pick(n, 1024)
    return pl.pallas_call(
        _ada_kernel,
        grid=(depth, n // tn),
        in_specs=[
            pl.BlockSpec((SUBLANES, d), lambda l, j: (0, 0)),
            pl.BlockSpec((None, d, tn), lambda l, j: (l, 0, j)),
            pl.BlockSpec((None, 1, tn), lambda l, j: (l, 0, j)),
        ],
        out_specs=pl.BlockSpec((None, SUBLANES, tn), lambda l, j: (l, 0, j)),
        out_shape=jax.ShapeDtypeStruct((depth, SUBLANES, n), F32),
        compiler_params=_params("arbitrary", "arbitrary"),
        name="ada_modulation",
    )(c8, ada_w, ada_b.reshape(depth, 1, n))


def _modulate_into(h_ref, x_ref, sc_ref, sh_ref):
    h_ref[...] = (x_ref[...] * (1.0 + sc_ref[...]) + sh_ref[...]).astype(BF16)


def _mm_mod_kernel(x_ref, sc_ref, sh_ref, w_ref, o_ref, h_ref, *, act):
    @pl.when(pl.program_id(2) == 0)
    def _():
        _modulate_into(h_ref, x_ref, sc_ref, sh_ref)

    acc = jnp.dot(h_ref[...], w_ref[...], preferred_element_type=F32)
    if act == "relu2":
        acc = jnp.square(jnp.maximum(acc, 0.0))
    o_ref[...] = acc.astype(o_ref.dtype)


def _mod_matmul(x3, sc, sh, w, *, act, out_dtype, tm=1024, tn=1024):
    g_, r, d = x3.shape
    n = w.shape[1]
    tm, tn = _pick(r, tm), _pick(n, tn)
    ni = r // tm
    return pl.pallas_call(
        functools.partial(_mm_mod_kernel, act=act),
        grid=(g_, ni, n // tn),
        in_specs=[
            pl.BlockSpec((None, tm, d), lambda g, i, j: (g, i, 0)),
            pl.BlockSpec((None, 1, d), lambda g, i, j: (g, 0, 0)),
            pl.BlockSpec((None, 1, d), lambda g, i, j: (g, 0, 0)),
            pl.BlockSpec((d, tn), lambda g, i, j: (0, j)),
        ],
        out_specs=pl.BlockSpec((tm, tn), lambda g, i, j: (g * ni + i, j)),
        out_shape=jax.ShapeDtypeStruct((g_ * r, n), out_dtype),
        scratch_shapes=[pltpu.VMEM((tm, d), BF16)],
        compiler_params=_params("arbitrary", "arbitrary", "arbitrary"),
        name="mod_matmul_" + act,
    )(x3, sc, sh, w)


def _softplus(v):
    return jnp.maximum(v, 0.0) + jnp.log1p(jnp.exp(-jnp.abs(v)))


def _in_proj_kernel(x_ref, sc_ref, sh_ref, w_ref, wdt_ref, dtb_ref, o_ref, odt_ref, h_ref):
    @pl.when(pl.program_id(2) == 0)
    def _():
        _modulate_into(h_ref, x_ref, sc_ref, sh_ref)
        dt_raw = jnp.dot(h_ref[...], wdt_ref[...], preferred_element_type=F32)
        odt_ref[...] = _softplus(dt_raw + dtb_ref[...])

    o_ref[...] = jnp.dot(h_ref[...], w_ref[...], preferred_element_type=F32)


def _ssd_in_proj(x3, sc, sh, w_zx, w_dt, dt_bias, *, tm=1024, tn=1024):
    g_, r, d = x3.shape
    n = w_zx.shape[1]
    ndt = w_dt.shape[1]
    tm, tn = _pick(r, tm), _pick(n, tn)
    ni = r // tm
    return pl.pallas_call(
        _in_proj_kernel,
        grid=(g_, ni, n // tn),
        in_specs=[
            pl.BlockSpec((None, tm, d), lambda g, i, j: (g, i, 0)),
            pl.BlockSpec((None, 1, d), lambda g, i, j: (g, 0, 0)),
            pl.BlockSpec((None, 1, d), lambda g, i, j: (g, 0, 0)),
            pl.BlockSpec((d, tn), lambda g, i, j: (0, j)),
            pl.BlockSpec((d, ndt), lambda g, i, j: (0, 0)),
            pl.BlockSpec((1, ndt), lambda g, i, j: (0, 0)),
        ],
        out_specs=[
            pl.BlockSpec((tm, tn), lambda g, i, j: (g * ni + i, j)),
            pl.BlockSpec((tm, ndt), lambda g, i, j: (g * ni + i, 0)),
        ],
        out_shape=[
            jax.ShapeDtypeStruct((g_ * r, n), F32),
            jax.ShapeDtypeStruct((g_ * r, ndt), F32),
        ],
        scratch_shapes=[pltpu.VMEM((tm, d), BF16)],
        compiler_params=_params("arbitrary", "arbitrary", "arbitrary"),
        name="ssd_in_proj",
    )(x3, sc, sh, w_zx, w_dt, dt_bias)


def _rope(v, cos, sin, first_half):
    dh = v.shape[1]
    partner = jnp.where(first_half, pltpu.roll(v, dh - dh // 4, 1), pltpu.roll(v, dh // 4, 1))
    return v * cos + partner * sin


def _qkv_kernel(x_ref, sc_ref, sh_ref, w_ref, cos_ref, sin_ref, o_ref, h_ref, *, nq, nk, dh, scale):
    j = pl.program_id(2)

    @pl.when(j == 0)
    def _():
        _modulate_into(h_ref, x_ref, sc_ref, sh_ref)

    acc = jnp.dot(h_ref[...], w_ref[...], preferred_element_type=F32)
    tn = acc.shape[1]

    def roped(mult):
        cos, sin = cos_ref[...], sin_ref[...]
        lane = lax.broadcasted_iota(jnp.int32, cos.shape, 1)
        first_half = (lane % (dh // 2)) < (dh // 4)
        parts = []
        for k in range(tn // dh):
            r = _rope(acc[:, k * dh:(k + 1) * dh], cos, sin, first_half)
            parts.append(r * mult if mult is not None else r)
        return jnp.concatenate(parts, axis=1)

    @pl.when(j < nq)
    def _():
        o_ref[...] = roped(scale).astype(o_ref.dtype)

    @pl.when((j >= nq) & (j < nq + nk))
    def _():
        o_ref[...] = roped(None).astype(o_ref.dtype)

    @pl.when(j >= nq + nk)
    def _():
        o_ref[...] = acc.astype(o_ref.dtype)


def _qkv_proj(x3, sc, sh, w, cos, sin, *, dh, tm=1024, tn=1024):
    g_, r, d = x3.shape
    n = w.shape[1]
    tm, tn = _pick(r, tm), _pick(d, tn)
    ni = r // tm
    return pl.pallas_call(
        functools.partial(_qkv_kernel, nq=d // tn, nk=d // tn, dh=dh, scale=dh ** -0.5),
        grid=(g_, ni, n // tn),
        in_specs=[
            pl.BlockSpec((None, tm, d), lambda g, i, j: (g, i, 0)),
            pl.BlockSpec((None, 1, d), lambda g, i, j: (g, 0, 0)),
            pl.BlockSpec((None, 1, d), lambda g, i, j: (g, 0, 0)),
            pl.BlockSpec((d, tn), lambda g, i, j: (0, j)),
            pl.BlockSpec((tm, dh), lambda g, i, j: (i, 0)),
            pl.BlockSpec((tm, dh), lambda g, i, j: (i, 0)),
        ],
        out_specs=pl.BlockSpec((tm, tn), lambda g, i, j: (g * ni + i, j)),
        out_shape=jax.ShapeDtypeStruct((g_ * r, n), BF16),
        scratch_shapes=[pltpu.VMEM((tm, d), BF16)],
        compiler_params=_params("arbitrary", "arbitrary", "arbitrary"),
        name="qkv_proj",
    )(x3, sc, sh, w, cos, sin)


def _res_ln(acc, x_ref, g_ref, lng_ref, lnb_ref, o_ref, alpha):
    r = alpha * x_ref[...] + g_ref[...] * acc
    mu = jnp.mean(r, axis=-1, keepdims=True)
    rc = r - mu
    var = jnp.mean(jnp.square(rc), axis=-1, keepdims=True)
    o_ref[...] = (rc * lax.rsqrt(var + LN_EPS)) * lng_ref[...] + lnb_ref[...]


def _mm_res_ln_kernel(a_ref, w_ref, x_ref, g_ref, lng_ref, lnb_ref, o_ref, acc_ref, *, nk, alpha):
    k = pl.program_id(2)
    part = jnp.dot(a_ref[...], w_ref[...], preferred_element_type=F32)

    @pl.when(k == 0)
    def _():
        acc_ref[...] = part

    @pl.when(k > 0)
    def _():
        acc_ref[...] += part

    @pl.when(k == nk - 1)
    def _():
        _res_ln(acc_ref[...], x_ref, g_ref, lng_ref, lnb_ref, o_ref, alpha)


def _matmul_res_ln(a, w, x3, gate, ln_g, ln_b, *, alpha, tm=512, tk=1024):
    g_, r, d = x3.shape
    kdim = a.shape[1]
    tm, tk = _pick(r, tm), _pick(kdim, tk)
    ni, nk = r // tm, kdim // tk
    return pl.pallas_call(
        functools.partial(_mm_res_ln_kernel, nk=nk, alpha=alpha),
        grid=(g_, ni, nk),
        in_specs=[
            pl.BlockSpec((tm, tk), lambda g, i, k: (g * ni + i, k)),
            pl.BlockSpec((tk, d), lambda g, i, k: (k, 0)),
            pl.BlockSpec((None, tm, d), lambda g, i, k: (g, i, 0)),
            pl.BlockSpec((None, 1, d), lambda g, i, k: (g, 0, 0)),
            pl.BlockSpec((1, d), lambda g, i, k: (0, 0)),
            pl.BlockSpec((1, d), lambda g, i, k: (0, 0)),
        ],
        out_specs=pl.BlockSpec((None, tm, d), lambda g, i, k: (g, i, 0)),
        out_shape=jax.ShapeDtypeStruct((g_, r, d), F32),
        scratch_shapes=[pltpu.VMEM((tm, d), F32)],
        compiler_params=_params("arbitrary", "arbitrary", "arbitrary"),
        name="matmul_res_ln",
    )(a, w, x3, gate, ln_g.reshape(1, d), ln_b.reshape(1, d))


def _conv_silu_kernel(main_ref, prev_ref, next_ref, w_ref, b_ref, o_ref, s_ref, *, tiles_per_seq):
    t = pl.program_id(1) % tiles_per_seq
    tl = main_ref.shape[0]
    halo = SUBLANES
    pad = (SSM_CONV - 1) // 2
    s_ref[0:halo, :] = jnp.where(t > 0, prev_ref[...], 0.0)
    s_ref[halo:halo + tl, :] = main_ref[...]
    s_ref[halo + tl:2 * halo + tl, :] = jnp.where(t < tiles_per_seq - 1, next_ref[...], 0.0)
    acc = jnp.broadcast_to(b_ref[...], main_ref.shape)
    for k in range(SSM_CONV):
        off = halo - pad + k
        acc = acc + s_ref[off:off + tl, :] * w_ref[k:k + 1, :]
    res = (acc * jax.nn.sigmoid(acc)).astype(o_ref.dtype)
    nsub, _, w = o_ref.shape
    for j in range(nsub):
        o_ref[j] = res[:, j * w:(j + 1) * w]


def _conv_silu(zx, conv_w, conv_b, *, zcol0, wcol0, ncols, sub_w, seq_len, tl=512, tc=512):
    m = zx.shape[0]
    cdim = conv_w.shape[1]
    tl = _pick(seq_len, tl)
    tps = seq_len // tl
    zb0, wb0 = zcol0 // tc, wcol0 // tc
    nsub = tc // sub_w
    hb = tl // SUBLANES
    last_hb = m // SUBLANES - 1
    return pl.pallas_call(
        functools.partial(_conv_silu_kernel, tiles_per_seq=tps),
        grid=(ncols // tc, m // tl),
        in_specs=[
            pl.BlockSpec((tl, tc), lambda cb, i: (i, zb0 + cb)),
            pl.BlockSpec((SUBLANES, tc), lambda cb, i: (jnp.maximum(i * hb - 1, 0), zb0 + cb)),
            pl.BlockSpec((SUBLANES, tc), lambda cb, i: (jnp.minimum((i + 1) * hb, last_hb), zb0 + cb)),
            pl.BlockSpec((SSM_CONV, tc), lambda cb, i: (0, wb0 + cb)),
            pl.BlockSpec((1, tc), lambda cb, i: (0, wb0 + cb)),
        ],
        out_specs=pl.BlockSpec((nsub, tl, sub_w), lambda cb, i: (cb, i, 0)),
        out_shape=jax.ShapeDtypeStruct((ncols // sub_w, m, sub_w), BF16),
        scratch_shapes=[pltpu.VMEM((tl + 2 * SUBLANES, tc), F32)],
        compiler_params=_params("arbitrary", "arbitrary"),
        name="conv_silu",
    )(zx, zx, zx, conv_w, conv_b.reshape(1, cdim))


def _split3(v):
    a1 = v.astype(BF16)
    r1 = v - a1.astype(F32)
    a2 = r1.astype(BF16)
    a3 = (r1 - a2.astype(F32)).astype(BF16)
    return a1, a2, a3


def _ssd_cumsum(dt_ref, a_col, csr_ref, cols_ref, *, reverse):
    q = SSM_CHUNK
    nh = dt_ref.shape[0]
    ii = lax.broadcasted_iota(jnp.int32, (q, q), 0)
    jj = lax.broadcasted_iota(jnp.int32, (q, q), 1)
    tri_t = ((ii >= jj) if reverse else (ii <= jj)).astype(BF16)
    dt_row = dt_ref[...]
    da_row = dt_row * a_col
    cs_row = sum(jnp.dot(t, tri_t, preferred_element_type=F32) for t in _split3(da_row))
    csr_ref[...] = cs_row
    pieces = [cs_row, dt_row]
    if q > 2 * nh:
        pieces.append(jnp.zeros((q - 2 * nh, q), F32))
    cols_ref[...] = jnp.concatenate(pieces, axis=0).T


def _ssd_group(g, x_ref, b_ref, c_ref, csr_ref, cols_ref, s_ref, y_ref, *, reverse, nh, hg, p):
    q = SSM_CHUNK
    ii = lax.broadcasted_iota(jnp.int32, (q, q), 0)
    jj = lax.broadcasted_iota(jnp.int32, (q, q), 1)
    causal = (ii <= jj) if reverse else (ii >= jj)

    x = x_ref[g]
    bm = b_ref[g]
    cm = c_ref[g]
    cs_row = csr_ref[pl.ds(pl.multiple_of(g * hg, hg), hg), :]
    cols = pltpu.roll(cols_ref[...], (q - g * hg) % q, 1)

    cb = lax.dot_general(cm, bm, (((1,), (1,)), ((), ())), preferred_element_type=F32)
    s_old = s_ref[g]
    c_state = jnp.dot(cm, s_old.astype(BF16), preferred_element_type=F32)
    end = 0 if reverse else q - 1

    pw = 2 * p
    first = lax.broadcasted_iota(jnp.int32, (q, pw), 1) < p
    ys, xws, decays = [], [], []
    for k in range(hg // 2):
        heads = (2 * k, 2 * k + 1)
        cs_b = [jnp.broadcast_to(cols[:, h:h + 1], (q, q)) for h in heads]
        dt_b = [jnp.broadcast_to(cols[:, nh + h:nh + h + 1], (q, pw)) for h in heads]
        ws = []
        for h, cs_i in zip(heads, cs_b):
            seg = cs_i - cs_row[h:h + 1, :]
            ws.append((cb * jnp.exp(jnp.where(causal, seg, -jnp.inf))).astype(BF16))
        cs_pair = jnp.where(first, cs_b[0][:, :pw], cs_b[1][:, :pw])
        dt_pair = jnp.where(first, dt_b[0], dt_b[1])
        xdt = x[:, k * pw:(k + 1) * pw].astype(F32) * dt_pair
        rhs = jnp.concatenate([jnp.where(first, xdt, 0.0).astype(BF16),
                               jnp.where(first, 0.0, xdt).astype(BF16)], axis=0)
        y_diag = jnp.dot(jnp.concatenate(ws, axis=1), rhs, preferred_element_type=F32)
        ys.append(y_diag + c_state[:, k * pw:(k + 1) * pw] * jnp.exp(cs_pair))
        cs_end = cs_pair[end:end + 1, :]
        xws.append((xdt * jnp.exp(cs_end - cs_pair)).astype(BF16))
        decays.append(jnp.exp(cs_end))
    y_ref[g] = jnp.concatenate(ys, axis=1)
    st = lax.dot_general(bm, jnp.concatenate(xws, axis=1), (((0,), (0,)), ((), ())),
                         preferred_element_type=F32)
    s_ref[g] = jnp.concatenate(decays, axis=1) * s_old + st


def _ssd_kernel(xf_ref, xb_ref, bf_ref, cf_ref, bb_ref, cb_ref, dtf_ref, dtb_ref, alog_ref,
                h0f_ref, h0b_ref, yf_ref, yb_ref, hff_ref, hfb_ref,
                sf_ref, sb_ref, csrf_ref, csrb_ref, colsf_ref, colsb_ref, *, nc, hg, p):
    c = pl.program_id(1)
    n_groups = xf_ref.shape[0]
    nh = dtf_ref.shape[0]

    @pl.when(c == 0)
    def _():
        sf_ref[...] = h0f_ref[...]
        sb_ref[...] = h0b_ref[...]

    a = -jnp.exp(alog_ref[...])
    _ssd_cumsum(dtf_ref, a[0], csrf_ref, colsf_ref, reverse=False)
    _ssd_cumsum(dtb_ref, a[1], csrb_ref, colsb_ref, reverse=True)

    def body(g, carry):
        _ssd_group(g, xf_ref, bf_ref, cf_ref, csrf_ref, colsf_ref, sf_ref, yf_ref,
                   reverse=False, nh=nh, hg=hg, p=p)
        _ssd_group(g, xb_ref, bb_ref, cb_ref, csrb_ref, colsb_ref, sb_ref, yb_ref,
                   reverse=True, nh=nh, hg=hg, p=p)
        return carry

    lax.fori_loop(0, n_groups, body, 0)

    @pl.when(c == nc - 1)
    def _():
        hff_ref[...] = sf_ref[...]
        hfb_ref[...] = sb_ref[...]


def _ssd_scan(xc, bc, dt_rows, a_log, h0f, h0b, *, n_seq, seq_len, p):
    g_, m, gw = xc.shape
    n, q = SSM_STATE, SSM_CHUNK
    nh = dt_rows.shape[1]
    hg = gw // p
    nc = seq_len // q

    def row_f(b, c):
        return b * nc + c

    def row_b(b, c):
        return b * nc + nc - 1 - c

    def xspec(row):
        return pl.BlockSpec((g_, q, gw), lambda b, c: (0, row(b, c), 0))

    def bcspec(row, k):
        return pl.BlockSpec((g_, q, n), lambda b, c: (k, row(b, c), 0))

    def dtspec(row, d):
        return pl.BlockSpec((None, nh, q), lambda b, c: (d, 0, row(b, c)))

    state_spec = pl.BlockSpec((None, g_, n, gw), lambda b, c: (b, 0, 0, 0))
    state_shape = jax.ShapeDtypeStruct((n_seq, g_, n, gw), F32)
    y_shape = jax.ShapeDtypeStruct((g_, m, gw), F32)
    return pl.pallas_call(
        functools.partial(_ssd_kernel, nc=nc, hg=hg, p=p),
        grid=(n_seq, nc),
        in_specs=[
            xspec(row_f), xspec(row_b),
            bcspec(row_f, 0), bcspec(row_f, 1), bcspec(row_b, 2), bcspec(row_b, 3),
            dtspec(row_f, 0), dtspec(row_b, 1),
            pl.BlockSpec((2, nh, 1), lambda b, c: (0, 0, 0)),
            state_spec, state_spec,
        ],
        out_specs=[xspec(row_f), xspec(row_b), state_spec, state_spec],
        out_shape=[y_shape, y_shape, state_shape, state_shape],
        scratch_shapes=[
            pltpu.VMEM((g_, n, gw), F32), pltpu.VMEM((g_, n, gw), F32),
            pltpu.VMEM((nh, q), F32), pltpu.VMEM((nh, q), F32),
            pltpu.VMEM((q, q), F32), pltpu.VMEM((q, q), F32),
        ],
        compiler_params=_params("arbitrary", "arbitrary"),
        name="ssd_scan",
    )(xc, xc, bc, bc, bc, bc, dt_rows, dt_rows, a_log, h0f, h0b)


def _ssd_gate_kernel(yf_ref, yb_ref, x_ref, z_ref, d_ref, nw_ref, o_ref):
    n_groups, tm, gw = yf_ref.shape
    d = d_ref[0:1, :] + d_ref[1:2, :]
    gated = []
    ss = jnp.zeros((tm, 1), F32)
    for g in range(n_groups):
        cols = slice(g * gw, (g + 1) * gw)
        y = yf_ref[g] + yb_ref[g] + d[:, cols] * x_ref[g].astype(F32)
        z = z_ref[:, cols]
        v = y * (z * jax.nn.sigmoid(z))
        ss = ss + jnp.sum(jnp.square(v), axis=-1, keepdims=True)
        gated.append(v)
    inv = lax.rsqrt(ss / (n_groups * gw) + LN_EPS)
    for g in range(n_groups):
        cols = slice(g * gw, (g + 1) * gw)
        o_ref[:, cols] = ((gated[g] * inv) * nw_ref[:, cols]).astype(o_ref.dtype)


def _ssd_gate(yf, yb, xc, zx, d_rep, norm_w, *, tm=128):
    g_, m, gw = yf.shape
    d_inner = g_ * gw
    tm = _pick(m, tm)
    grouped = pl.BlockSpec((g_, tm, gw), lambda i: (0, i, 0))
    return pl.pallas_call(
        _ssd_gate_kernel,
        grid=(m // tm,),
        in_specs=[
            grouped, grouped, grouped,
            pl.BlockSpec((tm, d_inner), lambda i: (i, 0)),
            pl.BlockSpec((2, d_inner), lambda i: (0, 0)),
            pl.BlockSpec((1, d_inner), lambda i: (0, 0)),
        ],
        out_specs=pl.BlockSpec((tm, d_inner), lambda i: (i, 0)),
        out_shape=jax.ShapeDtypeStruct((m, d_inner), BF16),
        compiler_params=_params("arbitrary"),
        name="ssd_gate_norm",
    )(yf, yb, xc, zx, d_rep, norm_w.reshape(1, d_inner))


def _attn_kernel(q_ref, kl_ref, vl_ref, kc_ref, vc_ref, lam_ref, sw_ref, o_ref, *, dh, lambda_init):
    nt = (((1,), (1,)), ((), ()))
    lf = lam_ref[...]
    lam = (jnp.exp(jnp.sum(lf[0:1] * lf[1:2], axis=-1, keepdims=True))
           - jnp.exp(jnp.sum(lf[2:3] * lf[3:4], axis=-1, keepdims=True)) + lambda_init)

    def softmax_parts(lo):
        qs = q_ref[:, lo:lo + dh]
        sl = lax.dot_general(qs, kl_ref[:, lo:lo + dh], nt, preferred_element_type=F32)
        sc = lax.dot_general(qs, kc_ref[:, lo:lo + dh], nt, preferred_element_type=F32)
        mx = jnp.maximum(jnp.max(sl, axis=-1, keepdims=True), jnp.max(sc, axis=-1, keepdims=True))
        el, ec = jnp.exp(sl - mx), jnp.exp(sc - mx)
        inv = 1.0 / (jnp.sum(el, axis=-1, keepdims=True) + jnp.sum(ec, axis=-1, keepdims=True))
        return el, ec, inv

    e1l, e1c, i1 = softmax_parts(0)
    e2l, e2c, i2 = softmax_parts(dh)
    i2 = lam * i2
    pl_ = (e1l * i1 - e2l * i2).astype(BF16)
    pc_ = (e1c * i1 - e2c * i2).astype(BF16)
    o = (jnp.dot(pl_, vl_ref[...], preferred_element_type=F32)
         + jnp.dot(pc_, vc_ref[...], preferred_element_type=F32))
    ms = jnp.mean(jnp.square(o), axis=-1, keepdims=True)
    o = (o * lax.rsqrt(ms + LN_EPS)) * sw_ref[...] * (1.0 - lambda_init)
    o_ref[...] = o.astype(o_ref.dtype)


def _diff_attention(qkv, kv_ctx, lam_qk, subln_w, *, n_seq, seq_len, ctx_len, d, dh, lambda_init, tq=256):
    hw = 2 * dh
    nh = d // hw
    tq = _pick(seq_len, tq)
    nq = seq_len // tq
    return pl.pallas_call(
        functools.partial(_attn_kernel, dh=dh, lambda_init=lambda_init),
        grid=(n_seq, nh, nq),
        in_specs=[
            pl.BlockSpec((tq, hw), lambda b, h, i: (b * nq + i, h)),
            pl.BlockSpec((seq_len, hw), lambda b, h, i: (b, nh + h)),
            pl.BlockSpec((seq_len, hw), lambda b, h, i: (b, 2 * nh + h)),
            pl.BlockSpec((ctx_len, hw), lambda b, h, i: (b, h)),
            pl.BlockSpec((ctx_len, hw), lambda b, h, i: (b, nh + h)),
            pl.BlockSpec((4, dh), lambda b, h, i: (0, 0)),
            pl.BlockSpec((1, hw), lambda b, h, i: (0, 0)),
        ],
        out_specs=pl.BlockSpec((tq, hw), lambda b, h, i: (b * nq + i, h)),
        out_shape=jax.ShapeDtypeStruct((n_seq * seq_len, d), BF16),
        compiler_params=_params("arbitrary", "arbitrary", "arbitrary"),
        name="diff_attention",
    )(qkv, qkv, qkv, kv_ctx, kv_ctx, lam_qk, subln_w.reshape(1, hw))


def _rope_tables(seq_len, dh):
    t = jnp.arange(seq_len)
    row = (t // GRID_W).astype(F32)
    col = (t % GRID_W).astype(F32)
    half = dh // 2
    inv_freq = ROPE_THETA ** (-jnp.arange(0, half, 2, dtype=F32) / half)
    ang_r, ang_c = row[:, None] * inv_freq, col[:, None] * inv_freq
    cos = jnp.concatenate([jnp.cos(ang_r)] * 2 + [jnp.cos(ang_c)] * 2, axis=-1)
    sin = jnp.concatenate([-jnp.sin(ang_r), jnp.sin(ang_r), -jnp.sin(ang_c), jnp.sin(ang_c)], axis=-1)
    return cos, sin


def _ssd_mixer(streams, w_in, conv_w, conv_b, dt_bias, a_log, d_skip, norm_w, *, p=64):
    d_inner = norm_w.shape[0]
    cdim = conv_w.shape[1]
    n_heads = a_log.shape[1]
    g_ = SSM_GROUPS
    hg = n_heads // g_
    gw = d_inner // g_
    w_zx = w_in[:, :d_inner + cdim].astype(BF16)
    w_dt = w_in[:, d_inner + cdim:].astype(BF16)
    dtb = dt_bias.reshape(1, 2 * n_heads)
    alog = a_log.reshape(2, n_heads, 1)
    d_rep = jnp.repeat(d_skip, p, axis=1)
    outs = []
    h0f = h0b = None
    for x3, sc, sh, n_seq, seq_len in streams:
        zx, dt = _ssd_in_proj(x3, sc, sh, w_zx, w_dt, dtb)
        xc = _conv_silu(zx, conv_w, conv_b, zcol0=d_inner, wcol0=0, ncols=d_inner, sub_w=gw,
                        seq_len=seq_len)
        bc = _conv_silu(zx, conv_w, conv_b, zcol0=2 * d_inner, wcol0=d_inner, ncols=cdim - d_inner,
                        sub_w=SSM_STATE, seq_len=seq_len)
        dt_rows = dt.reshape(-1, 2, n_heads).transpose(1, 2, 0)
        if h0f is None:
            h0f = h0b = jnp.zeros((n_seq, g_, SSM_STATE, gw), F32)
        yf, yb, h0f, h0b = _ssd_scan(xc, bc, dt_rows, alog, h0f, h0b, n_seq=n_seq, seq_len=seq_len, p=p)
        outs.append(_ssd_gate(yf, yb, xc, zx, d_rep, norm_w))
    return outs


def kernel(x, c, ctx, c_ctx, ada_w, ada_b, ln_mix_g, ln_mix_b, mlp_w_up, mlp_w_down, ln_mlp_g, ln_mlp_b, ssm_w_in, ssm_conv_w, ssm_conv_b, ssm_dt_bias, ssm_a_log, ssm_d, ssm_norm_w, ssm_w_out, attn_w_qkv, attn_lambda, attn_subln_w, attn_w_out):
    bsz, seq, d = x.shape
    ctx_len = ctx.shape[1]
    depth = ada_w.shape[0]
    alpha = (2 * depth) ** 0.25
    dh = attn_lambda.shape[-1]
    assert bsz + 1 <= SUBLANES

    c8 = jnp.concatenate([c, c_ctx[None, :], jnp.zeros((SUBLANES - bsz - 1, d), F32)], axis=0)
    mod = _ada_modulation(c8, ada_w, ada_b).reshape(depth, SUBLANES, 6, d)

    lat = x
    cx = ctx.reshape(1, bsz * ctx_len, d)
    for i in range(depth):
        last = i == depth - 1
        sh1, sc1, g1, sh2, sc2, g2 = [mod[i, :bsz, k][:, None, :] for k in range(6)]
        sh1c, sc1c, g1c, sh2c, sc2c, g2c = [mod[i, bsz:bsz + 1, k][:, None, :] for k in range(6)]
        j = i // N_MIXERS
        if i % N_MIXERS == 0:
            streams = [(cx, sc1c, sh1c, bsz, ctx_len), (lat, sc1, sh1, bsz, seq)]
            y_ctx, y_lat = _ssd_mixer(streams, ssm_w_in[j], ssm_conv_w[j], ssm_conv_b[j], ssm_dt_bias[j],
                                      ssm_a_log[j], ssm_d[j], ssm_norm_w[j])
            w_out = ssm_w_out[j].astype(BF16)
        else:
            if not last:
                raise NotImplementedError("context queries are only needed when attention is not the last layer")
            lambda_init = 0.8 - 0.6 * math.exp(-0.3 * i)
            w_qkv = attn_w_qkv[j].astype(BF16)
            cos, sin = _rope_tables(seq, dh)
            qkv = _qkv_proj(lat, sc1, sh1, w_qkv, cos, sin, dh=dh)
            kv_ctx = _mod_matmul(cx, sc1c, sh1c, w_qkv[:, d:], act="none", out_dtype=BF16)
            y_lat = _diff_attention(qkv, kv_ctx, attn_lambda[j], attn_subln_w[j], n_seq=bsz, seq_len=seq,
                                    ctx_len=ctx_len, d=d, dh=dh, lambda_init=lambda_init)
            y_ctx = None
            w_out = attn_w_out[j].astype(BF16)
        w_up = mlp_w_up[i].astype(BF16)
        w_down = mlp_w_down[i].astype(BF16)

        def finish(x3, y, gate1, sc, sh, gate2):
            x3 = _matmul_res_ln(y, w_out, x3, gate1, ln_mix_g[i], ln_mix_b[i], alpha=alpha)
            u = _mod_matmul(x3, sc, sh, w_up, act="relu2", out_dtype=BF16)
            return _matmul_res_ln(u, w_down, x3, gate2, ln_mlp_g[i], ln_mlp_b[i], alpha=alpha)

        lat = finish(lat, y_lat, g1, sc2, sh2, g2)
        if not last:
            cx = finish(cx, y_ctx, g1c, sc2c, sh2c, g2c)
    return lat
```
